```python
import math
import jax, jax.numpy as jnp
from jax import lax
import numpy as np

D_MODEL = 1024
BATCH = 4
SEQ = 8192
DEPTH = 1

CHUNK = 64
D_MIX = D_MODEL
GDN_HEADS = 4
GDN_HEAD_DIM = 128
GDN_WIDTH = GDN_HEADS * GDN_HEAD_DIM
SC_WIDTH = D_MIX - GDN_WIDTH
SC_GROUPS = 8
GDN_CONV = 4
SC_CONV = 3
D_FF = -(-8 * D_MODEL // (3 * 256)) * 256
RMS_EPS = 1e-6
L2_EPS = 1e-6

Q_OFF = 0
K_OFF = GDN_WIDTH
V_OFF = 2 * GDN_WIDTH
Z_OFF = 3 * GDN_WIDTH
A_OFF = 4 * GDN_WIDTH
BETA_OFF = A_OFF + GDN_HEADS
SCB_OFF = BETA_OFF + GDN_HEADS
SCC_OFF = SCB_OFF + SC_WIDTH
SCH_OFF = SCC_OFF + SC_WIDTH
D_IN = SCH_OFF + SC_WIDTH

kernel_name = "hybrid_gdn_shortconv_adaln_block"


def rms_norm(x, w):
    x32 = x.astype(jnp.float32)
    y = x32 * lax.rsqrt(jnp.mean(x32 * x32, axis=-1, keepdims=True) + RMS_EPS)
    return (y * w.astype(jnp.float32)).astype(x.dtype)


def l2_normalize(x):
    x32 = x.astype(jnp.float32)
    return x32 * lax.rsqrt(jnp.sum(x32 * x32, axis=-1, keepdims=True) + L2_EPS)


def causal_depthwise_conv(x, w):
    k_width = w.shape[0]
    s = x.shape[1]
    xp = jnp.pad(x, ((0, 0), (k_width - 1, 0), (0, 0)))
    out = xp[:, 0:s] * w[0]
    for j in range(1, k_width):
        out = out + xp[:, j:j + s] * w[j]
    return out


def gated_delta_rule_chunked(q, k, v, g, beta):
    bsz, s, h, dk = q.shape
    dv = v.shape[-1]
    n = s // CHUNK
    f32 = jnp.float32

    def to_chunks(t):
        return t.astype(f32).reshape(bsz, n, CHUNK, h, -1).transpose(0, 3, 1, 2, 4)

    q = to_chunks(q) * (dk ** -0.5)
    k = to_chunks(k)
    v = to_chunks(v)
    g = g.astype(f32).reshape(bsz, n, CHUNK, h).transpose(0, 3, 1, 2)
    beta = beta.astype(f32).reshape(bsz, n, CHUNK, h).transpose(0, 3, 1, 2)
    g = jnp.cumsum(g, axis=-1)

    k_beta = k * beta[..., None]
    v_beta = v * beta[..., None]
    tril = jnp.tril(jnp.ones((CHUNK, CHUNK), dtype=bool))
    strict = jnp.tril(jnp.ones((CHUNK, CHUNK), dtype=bool), -1)
    decay = jnp.exp(jnp.where(tril, g[..., :, None] - g[..., None, :], -jnp.inf))

    lmat = jnp.where(strict, jnp.einsum('bhncd,bhnmd->bhncm', k_beta, k) * decay, 0.0)
    eye = jnp.broadcast_to(jnp.eye(CHUNK, dtype=f32), lmat.shape)
    tmat = lax.linalg.triangular_solve(lmat, eye, left_side=True, lower=True, unit_diagonal=True)

    u = jnp.einsum('bhncm,bhnme->bhnce', tmat, v_beta)
    w = jnp.einsum('bhncm,bhnmd->bhncd', tmat, k_beta * jnp.exp(g)[..., None])
    attn = jnp.where(tril, jnp.einsum('bhncd,bhnmd->bhncm', q, k) * decay, 0.0)
    g_last = g[..., -1]
    k_dec = k * jnp.exp(g_last[..., None] - g)[..., None]
    q_dec = q * jnp.exp(g)[..., None]

    def step(state, inp):
        qd, wi, ui, ai, kd, gl = inp
        v_new = ui - jnp.einsum('bhcd,bhde->bhce', wi, state)
        out = jnp.einsum('bhcd,bhde->bhce', qd, state) + jnp.einsum('bhcm,bhme->bhce', ai, v_new)
        state = state * jnp.exp(gl)[..., None, None] + jnp.einsum('bhcd,bhce->bhde', kd, v_new)
        return state, out

    xs = tuple(jnp.moveaxis(t, 2, 0) for t in (q_dec, w, u, attn, k_dec, g_last))
    state0 = jnp.zeros((bsz, h, dk, dv), dtype=f32)
    _, outs = lax.scan(step, state0, xs)
    return outs.transpose(1, 0, 3, 2, 4).reshape(bsz, s, h, dv)


def hybrid_mixer(h, w_in, conv_qkv_w, a_log, dt_bias, gdn_norm_w, conv_short_w, w_out):
    bsz, s, _ = h.shape
    p = h @ w_in
    qkv = jax.nn.silu(causal_depthwise_conv(p[..., Q_OFF:Z_OFF], conv_qkv_w))
    q = qkv[..., 0:GDN_WIDTH].reshape(bsz, s, GDN_HEADS, GDN_HEAD_DIM)
    k = qkv[..., GDN_WIDTH:2 * GDN_WIDTH].reshape(bsz, s, GDN_HEADS, GDN_HEAD_DIM)
    v = qkv[..., 2 * GDN_WIDTH:3 * GDN_WIDTH].reshape(bsz, s, GDN_HEADS, GDN_HEAD_DIM)
    z = p[..., Z_OFF:A_OFF].reshape(bsz, s, GDN_HEADS, GDN_HEAD_DIM)
    a = p[..., A_OFF:BETA_OFF].astype(jnp.float32)
    b = p[..., BETA_OFF:SCB_OFF].astype(jnp.float32)
    g = -jnp.exp(a_log.astype(jnp.float32)) * jax.nn.softplus(a + dt_bias.astype(jnp.float32))
    beta = jax.nn.sigmoid(b)
    o = gated_delta_rule_chunked(l2_normalize(q), l2_normalize(k), v, g, beta)
    o = rms_norm(o, gdn_norm_w) * jax.nn.silu(z.astype(jnp.float32))
    y_gdn = o.reshape(bsz, s, GDN_WIDTH).astype(h.dtype)
    sc_b = p[..., SCB_OFF:SCC_OFF]
    sc_c = p[..., SCC_OFF:SCH_OFF]
    sc_h = p[..., SCH_OFF:D_IN]
    y_sc = sc_b * causal_depthwise_conv(sc_c * sc_h, conv_short_w)
    return jnp.concatenate([y_gdn, y_sc], axis=-1) @ w_out


def swiglu(h, w_gate, w_up, w_down):
    return (jax.nn.silu(h @ w_gate) * (h @ w_up)) @ w_down


def setup_inputs(seed: int = 0) -> dict:
    key = jax.random.key(seed)
    ks = jax.random.split(key, 20)
    f32 = jnp.float32
    nrm = lambda k, shape, scale: jax.random.normal(k, shape, f32) * scale
    dt = jnp.exp(jax.random.uniform(ks[8], (DEPTH, GDN_HEADS), f32, math.log(1e-3), math.log(1e-1)))
    return {
        "x": nrm(ks[0], (BATCH, SEQ, D_MODEL), 1.0),
        "c": nrm(ks[1], (BATCH, D_MODEL), 1.0),
        "w_ada": nrm(ks[2], (DEPTH, D_MODEL, 6 * D_MODEL), 0.5 * D_MODEL ** -0.5),
        "b_ada": nrm(ks[3], (DEPTH, 6 * D_MODEL), 0.02),
        "norm1_w": 1.0 + nrm(ks[4], (DEPTH, D_MODEL), 0.02),
        "w_in": nrm(ks[5], (DEPTH, D_MODEL, D_IN), D_MODEL ** -0.5),
        "conv_qkv_w": nrm(ks[6], (DEPTH, GDN_CONV, 3 * GDN_WIDTH), GDN_CONV ** -0.5),
        "A_log": jnp.log(jax.random.uniform(ks[7], (DEPTH, GDN_HEADS), f32, 1.0, 16.0)),
        "dt_bias": dt + jnp.log(-jnp.expm1(-dt)),
        "gdn_norm_w": 1.0 + nrm(ks[9], (DEPTH, GDN_HEAD_DIM), 0.02),
        "conv_short_w": nrm(ks[10], (DEPTH, SC_CONV, SC_WIDTH), SC_CONV ** -0.5),
        "w_out": nrm(ks[11], (DEPTH, D_MIX, D_MODEL), D_MIX ** -0.5),
        "norm2_w": 1.0 + nrm(ks[12], (DEPTH, D_MODEL), 0.02),
        "w_gate": nrm(ks[13], (DEPTH, D_MODEL, D_FF), D_MODEL ** -0.5),
        "w_up": nrm(ks[14], (DEPTH, D_MODEL, D_FF), D_MODEL ** -0.5),
        "w_down": nrm(ks[15], (DEPTH, D_FF, D_MODEL), D_FF ** -0.5),
        "norm_f_w": 1.0 + nrm(ks[16], (D_MODEL,), 0.02),
    }


def reference(x, c, w_ada, b_ada, norm1_w, w_in, conv_qkv_w, A_log, dt_bias, gdn_norm_w,
              conv_short_w, w_out, norm2_w, w_gate, w_up, w_down, norm_f_w):
    c_act = jax.nn.silu(c)
    for l in range(DEPTH):
        mod = (c_act @ w_ada[l] + b_ada[l])[:, None, :]
        shift1, scale1, gate1, shift2, scale2, gate2 = jnp.split(mod, 6, axis=-1)
        h = rms_norm(x, norm1_w[l]) * (1.0 + scale1) + shift1
        x = x + gate1 * hybrid_mixer(h, w_in[l], conv_qkv_w[l], A_log[l], dt_bias[l],
                                     gdn_norm_w[l], conv_short_w[l], w_out[l])
        h = rms_norm(x, norm2_w[l]) * (1.0 + scale2) + shift2
        x = x + gate2 * swiglu(h, w_gate[l], w_up[l], w_down[l])
    return rms_norm(x, norm_f_w)
```

```python
import functools

import jax
import jax.numpy as jnp
from jax import lax
from jax.experimental import pallas as pl
from jax.experimental.pallas import tpu as pltpu

F32 = jnp.float32
BF16 = jnp.bfloat16

D_MODEL = 1024
GDN_HEADS = 4
GDN_HEAD_DIM = 128
GDN_WIDTH = GDN_HEADS * GDN_HEAD_DIM
SC_WIDTH = 512
D_FF = 2816
CHUNK = 64
GDN_CONV = 4
SC_CONV = 3
RMS_EPS = 1e-6
L2_EPS = 1e-6

LANES = 128
SUBLANES = 8
TM_MIX = 512
SUB = 256
TM_FFN = 512
FF_SPLIT = 2
ADA_TN = 1536
VMEM_LIMIT = 56 * 1024 * 1024
NEG_BIG = -1e30


def _dot(a, b):
    return jnp.dot(a, b, preferred_element_type=F32)


def _dot_nt(a, b):
    return lax.dot_general(a, b, (((1,), (1,)), ((), ())), preferred_element_type=F32)


def _dot_tn(a, b):
    return lax.dot_general(a, b, (((0,), (0,)), ((), ())), preferred_element_type=F32)


def _dot_exact_lhs(a_bf16, b_f32):
    b0 = b_f32.astype(BF16)
    r1 = b_f32 - b0.astype(F32)
    b1 = r1.astype(BF16)
    b2 = (r1 - b1.astype(F32)).astype(BF16)
    return _dot(a_bf16, b0) + _dot(a_bf16, b1) + _dot(a_bf16, b2)


def _silu(v):
    return v * jax.nn.sigmoid(v)


def _softplus(v):
    return jnp.maximum(v, 0.0) + jnp.log1p(jnp.exp(-jnp.abs(v)))


def _ada_kernel(c_ref, w_ref, b_ref, o_ref):
    c = c_ref[...]
    o_ref[...] = _dot(_silu(c).astype(BF16), w_ref[...].astype(BF16)) + b_ref[...]


def _mixer_kernel(x_ref, mod_ref, n1_ref, wqkv_ref, wz_ref, wsc_ref, wab_ref, cq_ref, cs_ref,
                  alog_ref, dtb_ref, gnw_ref, wout_ref, tril_ref, strict_ref, tri16_ref,
                  o_ref, pq_buf, sc_buf, z_buf, scb_buf, ab_buf, y_buf, state_ref):
    tm = x_ref.shape[0]
    step = pl.program_id(1)

    @pl.when(step == 0)
    def _():
        pq_buf[0:SUBLANES, :] = jnp.zeros((SUBLANES, pq_buf.shape[1]), F32)
        sc_buf[0:SUBLANES, :] = jnp.zeros((SUBLANES, sc_buf.shape[1]), F32)
        state_ref[...] = jnp.zeros(state_ref.shape, F32)

    @pl.when(step > 0)
    def _():
        pq_buf[0:SUBLANES, :] = pq_buf[tm:tm + SUBLANES, :]
        sc_buf[0:SUBLANES, :] = sc_buf[tm:tm + SUBLANES, :]

    x = x_ref[...]
    shift1 = mod_ref[0:1, :]
    scale1 = mod_ref[1:2, :]
    gate1 = mod_ref[2:3, :]
    ms = jnp.mean(x * x, axis=-1, keepdims=True)
    h = (x * lax.rsqrt(ms + RMS_EPS) * n1_ref[...]) * (1.0 + scale1) + shift1
    hb = h.astype(BF16)

    pq_buf[SUBLANES:tm + SUBLANES, :] = _dot(hb, wqkv_ref[...])
    z_buf[...] = _dot(hb, wz_ref[...])
    psc = _dot(hb, wsc_ref[...])
    scb_buf[...] = psc[:, 0:SC_WIDTH]
    sc_buf[SUBLANES:tm + SUBLANES, :] = psc[:, SC_WIDTH:2 * SC_WIDTH] * psc[:, 2 * SC_WIDTH:3 * SC_WIDTH]
    ab_buf[...] = _dot(hb, wab_ref[...])

    a_neg = -jnp.exp(alog_ref[...])
    dtb = dtb_ref[...]
    gnw = gnw_ref[...]
    tril = tril_ref[...]
    strict = strict_ref[...]
    eye = tril - strict
    n_chunks = SUB // CHUNK

    for s in range(tm // SUB):
        r0 = s * SUB
        conv = pq_buf[pl.ds(r0 + SUBLANES - (GDN_CONV - 1), SUB), :] * cq_ref[0:1, :]
        for j in range(1, GDN_CONV):
            conv = conv + pq_buf[pl.ds(r0 + SUBLANES - (GDN_CONV - 1) + j, SUB), :] * cq_ref[j:j + 1, :]
        qkv = _silu(conv)

        ab = ab_buf[r0:r0 + SUB, :]
        g = a_neg * _softplus(ab + dtb)
        beta_all = jax.nn.sigmoid(ab)
        cum = _dot_exact_lhs(tri16_ref[...], g)
        cum_t = cum.T
        ecum = jnp.exp(cum)
        cum_last = jnp.concatenate(
            [jnp.broadcast_to(cum[c * CHUNK + CHUNK - 1:c * CHUNK + CHUNK, :], (CHUNK, LANES))
             for c in range(n_chunks)], axis=0)
        e_to_last = jnp.exp(cum_last - cum)
        e_last = jnp.exp(cum_last)

        for hd in range(GDN_HEADS):
            c0 = hd * GDN_HEAD_DIM
            q = qkv[:, c0:c0 + GDN_HEAD_DIM]
            k = qkv[:, GDN_WIDTH + c0:GDN_WIDTH + c0 + GDN_HEAD_DIM]
            v = qkv[:, 2 * GDN_WIDTH + c0:2 * GDN_WIDTH + c0 + GDN_HEAD_DIM]
            qn = q * lax.rsqrt(jnp.sum(q * q, axis=-1, keepdims=True) + L2_EPS) * (GDN_HEAD_DIM ** -0.5)
            kn = k * lax.rsqrt(jnp.sum(k * k, axis=-1, keepdims=True) + L2_EPS)
            beta = beta_all[:, GDN_HEADS + hd:GDN_HEADS + hd + 1]
            eg = ecum[:, hd:hd + 1]
            kb = kn * beta
            vb = v * beta

            diff = cum[:, hd:hd + 1] - cum_t[hd:hd + 1, :]
            dec = jnp.exp(jnp.where(tril > 0.0, diff, NEG_BIG))
            kn16 = kn.astype(BF16)
            lmat = _dot_nt(kb.astype(BF16), kn16) * dec * strict
            attn = _dot_nt(qn.astype(BF16), kn16) * dec

            l16 = lmat.astype(BF16)
            tmat = eye - lmat
            pw = _dot(l16, l16)
            n_fac = CHUNK.bit_length() - 2
            for it in range(n_fac):
                pw16 = pw.astype(BF16)
                tmat = tmat + _dot(tmat.astype(BF16), pw16)
                if it + 1 < n_fac:
                    pw = _dot(pw16, pw16)

            rhs = jnp.concatenate([vb, kb * eg], axis=1).astype(BF16)
            uw = _dot(tmat.astype(BF16), rhs)
            u = uw[:, 0:GDN_HEAD_DIM]
            w = uw[:, GDN_HEAD_DIM:2 * GDN_HEAD_DIM]
            qd = qn * eg
            kd = kn * e_to_last[:, hd:hd + 1]

            st = state_ref[hd]
            v_news = []
            o_inter = []
            for c in range(n_chunks):
                lo = c * CHUNK
                lhs = jnp.concatenate([w[lo:lo + CHUNK], qd[lo:lo + CHUNK]], axis=0).astype(BF16)
                res = _dot(lhs, st.astype(BF16))
                v_new = u[lo:lo + CHUNK] - res[0:CHUNK]
                o_inter.append(res[CHUNK:2 * CHUNK])
                v_news.append(v_new)
                st = st * e_last[lo:lo + 1, hd:hd + 1] + _dot_tn(kd[lo:lo + CHUNK].astype(BF16), v_new.astype(BF16))
            state_ref[hd] = st

            o = jnp.concatenate(o_inter, axis=0) + _dot(attn.astype(BF16), jnp.concatenate(v_news, axis=0).astype(BF16))
            o = o * lax.rsqrt(jnp.mean(o * o, axis=-1, keepdims=True) + RMS_EPS) * gnw
            zz = z_buf[r0:r0 + SUB, c0:c0 + GDN_HEAD_DIM]
            y_buf[r0:r0 + SUB, c0:c0 + GDN_HEAD_DIM] = (o * _silu(zz)).astype(BF16)

        csum = sc_buf[pl.ds(r0 + SUBLANES - (SC_CONV - 1), SUB), :] * cs_ref[0:1, :]
        for j in range(1, SC_CONV):
            csum = csum + sc_buf[pl.ds(r0 + SUBLANES - (SC_CONV - 1) + j, SUB), :] * cs_ref[j:j + 1, :]
        y_buf[r0:r0 + SUB, GDN_WIDTH:GDN_WIDTH + SC_WIDTH] = (scb_buf[r0:r0 + SUB, :] * csum).astype(BF16)

    o_ref[...] = x + gate1 * _dot(y_buf[...], wout_ref[...])


def _ffn_kernel(x_ref, mod_ref, n2_ref, wg_ref, wu_ref, wd_ref, nf_ref, o_ref, a_buf):
    x = x_ref[...]
    shift2 = mod_ref[3:4, :]
    scale2 = mod_ref[4:5, :]
    gate2 = mod_ref[5:6, :]
    ms = jnp.mean(x * x, axis=-1, keepdims=True)
    h = (x * lax.rsqrt(ms + RMS_EPS) * n2_ref[...]) * (1.0 + scale2) + shift2
    hb = h.astype(BF16)
    wcols = D_FF // FF_SPLIT
    for c in range(FF_SPLIT):
        gt = _dot(hb, wg_ref[:, c * wcols:(c + 1) * wcols])
        up = _dot(hb, wu_ref[:, c * wcols:(c + 1) * wcols])
        a_buf[:, c * wcols:(c + 1) * wcols] = (_silu(gt) * up).astype(BF16)
    x2 = x + gate2 * _dot(a_buf[...], wd_ref[...])
    ms2 = jnp.mean(x2 * x2, axis=-1, keepdims=True)
    o_ref[...] = x2 * lax.rsqrt(ms2 + RMS_EPS) * nf_ref[...]


def _const_spec(shape):
    return pl.BlockSpec(shape, lambda *_: (0,) * len(shape))


def _adaln(c, w_ada, b_ada):
    bsz = c.shape[0]
    rows = -(-bsz // SUBLANES) * SUBLANES
    c_pad = jnp.zeros((rows, D_MODEL), F32).at[:bsz].set(c)
    n_out = w_ada.shape[1]
    mod = pl.pallas_call(
        _ada_kernel,
        grid=(n_out // ADA_TN,),
        in_specs=[_const_spec((rows, D_MODEL)),
                  pl.BlockSpec((D_MODEL, ADA_TN), lambda j: (0, j)),
                  pl.BlockSpec((1, ADA_TN), lambda j: (0, j))],
        out_specs=pl.BlockSpec((rows, ADA_TN), lambda j: (0, j)),
        out_shape=jax.ShapeDtypeStruct((rows, n_out), F32),
        compiler_params=pltpu.CompilerParams(dimension_semantics=("arbitrary",), vmem_limit_bytes=VMEM_LIMIT),
        name="adaln_mod",
    )(c_pad, w_ada, b_ada.reshape(1, n_out))
    return mod[:bsz].reshape(bsz, 6, D_MODEL)


def _mixer(x, mod, norm1_w, w_in, conv_qkv_w, a_log, dt_bias, gdn_norm_w, conv_short_w, w_out):
    bsz, seq, _ = x.shape
    gw = GDN_WIDTH
    w16 = w_in.astype(BF16)
    w_qkv = w16[:, 0:3 * gw]
    w_z = w16[:, 3 * gw:4 * gw]
    ab0 = 4 * gw
    w_ab = jnp.zeros((D_MODEL, LANES), BF16).at[:, 0:2 * GDN_HEADS].set(w16[:, ab0:ab0 + 2 * GDN_HEADS])
    sc0 = ab0 + 2 * GDN_HEADS
    w_sc = w16[:, sc0:sc0 + 3 * SC_WIDTH]
    a_log_p = jnp.zeros((1, LANES), F32).at[0, 0:GDN_HEADS].set(a_log)
    dtb_p = jnp.zeros((1, LANES), F32).at[0, 0:GDN_HEADS].set(dt_bias)

    ri = lax.broadcasted_iota(jnp.int32, (SUB, SUB), 0)
    ci = lax.broadcasted_iota(jnp.int32, (SUB, SUB), 1)
    same = (ri // CHUNK) == (ci // CHUNK)
    tril = jnp.where(same & (ri >= ci), 1.0, 0.0).astype(F32)
    strict = jnp.where(same & (ri > ci), 1.0, 0.0).astype(F32)

    tm = TM_MIX
    row_spec = pl.BlockSpec((None, tm, D_MODEL), lambda b, i: (b, i, 0))
    return pl.pallas_call(
        _mixer_kernel,
        grid=(bsz, seq // tm),
        in_specs=[row_spec,
                  pl.BlockSpec((None, 6, D_MODEL), lambda b, i: (b, 0, 0)),
                  _const_spec((1, D_MODEL)),
                  _const_spec((D_MODEL, 3 * gw)),
                  _const_spec((D_MODEL, gw)),
                  _const_spec((D_MODEL, 3 * SC_WIDTH)),
                  _const_spec((D_MODEL, LANES)),
                  _const_spec((GDN_CONV, 3 * gw)),
                  _const_spec((SC_CONV, SC_WIDTH)),
                  _const_spec((1, LANES)),
                  _const_spec((1, LANES)),
                  _const_spec((1, GDN_HEAD_DIM)),
                  _const_spec((D_MODEL, D_MODEL)),
                  _const_spec((SUB, SUB)),
                  _const_spec((SUB, SUB)),
                  _const_spec((SUB, SUB))],
        out_specs=row_spec,
        out_shape=jax.ShapeDtypeStruct(x.shape, F32),
        scratch_shapes=[pltpu.VMEM((tm + SUBLANES, 3 * gw), F32),
                        pltpu.VMEM((tm + SUBLANES, SC_WIDTH), F32),
                        pltpu.VMEM((tm, gw), F32),
                        pltpu.VMEM((tm, SC_WIDTH), F32),
                        pltpu.VMEM((tm, LANES), F32),
                        pltpu.VMEM((tm, D_MODEL), BF16),
                        pltpu.VMEM((GDN_HEADS, GDN_HEAD_DIM, GDN_HEAD_DIM), F32)],
        compiler_params=pltpu.CompilerParams(dimension_semantics=("arbitrary", "arbitrary"),
                                             vmem_limit_bytes=VMEM_LIMIT),
        name="mixer",
    )(x, mod, norm1_w.reshape(1, D_MODEL), w_qkv, w_z, w_sc, w_ab, conv_qkv_w, conv_short_w,
      a_log_p, dtb_p, gdn_norm_w.reshape(1, GDN_HEAD_DIM), w_out.astype(BF16), tril, strict, tril.astype(BF16))


def _ffn(x1, mod, norm2_w, w_gate, w_up, w_down, norm_f_w):
    bsz, seq, _ = x1.shape
    tm = TM_FFN
    row_spec = pl.BlockSpec((None, tm, D_MODEL), lambda b, i: (b, i, 0))
    return pl.pallas_call(
        _ffn_kernel,
        grid=(bsz, seq // tm),
        in_specs=[row_spec,
                  pl.BlockSpec((None, 6, D_MODEL), lambda b, i: (b, 0, 0)),
                  _const_spec((1, D_MODEL)),
                  _const_spec((D_MODEL, D_FF)),
                  _const_spec((D_MODEL, D_FF)),
                  _const_spec((D_FF, D_MODEL)),
                  _const_spec((1, D_MODEL))],
        out_specs=row_spec,
        out_shape=jax.ShapeDtypeStruct(x1.shape, F32),
        scratch_shapes=[pltpu.VMEM((tm, D_FF), BF16)],
        compiler_params=pltpu.CompilerParams(dimension_semantics=("parallel", "parallel"),
                                             vmem_limit_bytes=VMEM_LIMIT),
        name="ffn_final_norm",
    )(x1, mod, norm2_w.reshape(1, D_MODEL), w_gate.astype(BF16), w_up.astype(BF16), w_down.astype(BF16),
      norm_f_w.reshape(1, D_MODEL))


def kernel(x, c, w_ada, b_ada, norm1_w, w_in, conv_qkv_w, A_log, dt_bias, gdn_norm_w, conv_short_w, w_out,
           norm2_w, w_gate, w_up, w_down, norm_f_w):
    depth = w_ada.shape[0]
    for l in range(depth):
        mod = _adaln(c, w_ada[l], b_ada[l])
        x = _mixer(x, mod, norm1_w[l], w_in[l], conv_qkv_w[l], A_log[l], dt_bias[l], gdn_norm_w[l],
                   conv_short_w[l], w_out[l])
        last = l == depth - 1
        assert last, "only the final layer's feed-forward kernel carries the output norm"
        x = _ffn(x, mod, norm2_w[l], w_gate[l], w_up[l], w_down[l], norm_f_w)
    return x
```

```python
import functools

import jax
import jax.numpy as jnp
from jax import lax
from jax.experimental import pallas as pl
from jax.experimental.pallas import tpu as pltpu

F32 = jnp.float32
BF16 = jnp.bfloat16

D_MODEL = 1024
GDN_HEADS = 4
GDN_HEAD_DIM = 128
GDN_WIDTH = GDN_HEADS * GDN_HEAD_DIM
SC_WIDTH = 512
D_FF = 2816
CHUNK = 64
GDN_CONV = 4
SC_CONV = 3
RMS_EPS = 1e-6
L2_EPS = 1e-6

LANES = 128
SUBLANES = 8
TM_MIX = 512
SUB = 256
TM_FFN = 512
FF_SPLIT = 2
ADA_TN = 1536
VMEM_LIMIT = 56 * 1024 * 1024
NEG_BIG = -1e30


def _dot(a, b):
    return jnp.dot(a, b, preferred_element_type=F32)


def _dot_nt(a, b):
    return lax.dot_general(a, b, (((1,), (1,)), ((), ())), preferred_element_type=F32)


def _dot_tn(a, b):
    return lax.dot_general(a, b, (((0,), (0,)), ((), ())), preferred_element_type=F32)


def _dot_exact_lhs(a_bf16, b_f32):
    b0 = b_f32.astype(BF16)
    r1 = b_f32 - b0.astype(F32)
    b1 = r1.astype(BF16)
    b2 = (r1 - b1.astype(F32)).astype(BF16)
    return _dot(a_bf16, b0) + _dot(a_bf16, b1) + _dot(a_bf16, b2)


def _silu(v):
    return v * jax.nn.sigmoid(v)


def _softplus(v):
    return jnp.maximum(v, 0.0) + jnp.log1p(jnp.exp(-jnp.abs(v)))


def _ada_kernel(c_ref, w_ref, b_ref, o_ref):
    c = c_ref[...]
    o_ref[...] = _dot(_silu(c).astype(BF16), w_ref[...].astype(BF16)) + b_ref[...]


def _mixer_kernel(x_ref, mod_ref, n1_ref, wqkv_ref, wz_ref, wsc_ref, wab_ref, cq_ref, cs_ref,
                  alog_ref, dtb_ref, gnw_ref, wout_ref, tril_ref, strict_ref, tri16_ref,
                  o_ref, pq_buf, sc_buf, z_buf, scb_buf, ab_buf, y_buf, state_ref):
    tm = x_ref.shape[0]
    step = pl.program_id(1)

    @pl.when(step == 0)
    def _():
        pq_buf[0:SUBLANES, :] = jnp.zeros((SUBLANES, pq_buf.shape[1]), F32)
        sc_buf[0:SUBLANES, :] = jnp.zeros((SUBLANES, sc_buf.shape[1]), F32)
        state_ref[...] = jnp.zeros(state_ref.shape, F32)

    @pl.when(step > 0)
    def _():
        pq_buf[0:SUBLANES, :] = pq_buf[tm:tm + SUBLANES, :]
        sc_buf[0:SUBLANES, :] = sc_buf[tm:tm + SUBLANES, :]

    x = x_ref[...]
    shift1 = mod_ref[0:1, :]
    scale1 = mod_ref[1:2, :]
    gate1 = mod_ref[2:3, :]
    ms = jnp.mean(x * x, axis=-1, keepdims=True)
    h = (x * lax.rsqrt(ms + RMS_EPS) * n1_ref[...]) * (1.0 + scale1) + shift1
    hb = h.astype(BF16)

    pq_buf[SUBLANES:tm + SUBLANES, :] = _dot(hb, wqkv_ref[...])
    z_buf[...] = _dot(hb, wz_ref[...])
    psc = _dot(hb, wsc_ref[...])
    scb_buf[...] = psc[:, 0:SC_WIDTH]
    sc_buf[SUBLANES:tm + SUBLANES, :] = psc[:, SC_WIDTH:2 * SC_WIDTH] * psc[:, 2 * SC_WIDTH:3 * SC_WIDTH]
    ab_buf[...] = _dot(hb, wab_ref[...])

    a_neg = -jnp.exp(alog_ref[...])
    dtb = dtb_ref[...]
    gnw = gnw_ref[...]
    tril = tril_ref[...]
    strict = strict_ref[...]
    eye = tril - strict
    n_chunks = SUB // CHUNK

    n_sub = tm // SUB
    n_fac = CHUNK.bit_length() - 2
    units = [(s, hd) for s in range(n_sub) for hd in range(GDN_HEADS)]

    e_last = []
    per = {}
    for s in range(n_sub):
        r0 = s * SUB
        conv = pq_buf[pl.ds(r0 + SUBLANES - (GDN_CONV - 1), SUB), :] * cq_ref[0:1, :]
        for j in range(1, GDN_CONV):
            conv = conv + pq_buf[pl.ds(r0 + SUBLANES - (GDN_CONV - 1) + j, SUB), :] * cq_ref[j:j + 1, :]
        qkv = _silu(conv)

        csum = sc_buf[pl.ds(r0 + SUBLANES - (SC_CONV - 1), SUB), :] * cs_ref[0:1, :]
        for j in range(1, SC_CONV):
            csum = csum + sc_buf[pl.ds(r0 + SUBLANES - (SC_CONV - 1) + j, SUB), :] * cs_ref[j:j + 1, :]
        y_buf[r0:r0 + SUB, GDN_WIDTH:GDN_WIDTH + SC_WIDTH] = (scb_buf[r0:r0 + SUB, :] * csum).astype(BF16)

        ab = ab_buf[r0:r0 + SUB, :]
        g = a_neg * _softplus(ab + dtb)
        beta_all = jax.nn.sigmoid(ab)
        cum = _dot_exact_lhs(tri16_ref[...], g)
        cum_t = cum.T
        ecum = jnp.exp(cum)
        cum_last = jnp.concatenate(
            [jnp.broadcast_to(cum[c * CHUNK + CHUNK - 1:c * CHUNK + CHUNK, :], (CHUNK, LANES))
             for c in range(n_chunks)], axis=0)
        e_to_last = jnp.exp(cum_last - cum)
        e_last.append(jnp.exp(cum_last))

        for hd in range(GDN_HEADS):
            c0 = hd * GDN_HEAD_DIM
            q = qkv[:, c0:c0 + GDN_HEAD_DIM]
            k = qkv[:, GDN_WIDTH + c0:GDN_WIDTH + c0 + GDN_HEAD_DIM]
            v = qkv[:, 2 * GDN_WIDTH + c0:2 * GDN_WIDTH + c0 + GDN_HEAD_DIM]
            qn = q * lax.rsqrt(jnp.sum(q * q, axis=-1, keepdims=True) + L2_EPS) * (GDN_HEAD_DIM ** -0.5)
            kn = k * lax.rsqrt(jnp.sum(k * k, axis=-1, keepdims=True) + L2_EPS)
            beta = beta_all[:, GDN_HEADS + hd:GDN_HEADS + hd + 1]
            eg = ecum[:, hd:hd + 1]
            kb = kn * beta
            diff = cum[:, hd:hd + 1] - cum_t[hd:hd + 1, :]
            dec = jnp.exp(jnp.where(tril > 0.0, diff, NEG_BIG))
            kn16 = kn.astype(BF16)
            lmat = _dot_nt(kb.astype(BF16), kn16) * dec * strict
            per[s, hd] = dict(
                lmat=lmat,
                attn=(_dot_nt(qn.astype(BF16), kn16) * dec).astype(BF16),
                rhs=jnp.concatenate([v * beta, kb * eg], axis=1).astype(BF16),
                qd=(qn * eg).astype(BF16),
                kd=(kn * e_to_last[:, hd:hd + 1]).astype(BF16))

    tmat = {}
    pw16 = {}
    for key in units:
        l16 = per[key]["lmat"].astype(BF16)
        tmat[key] = eye - per[key]["lmat"]
        pw16[key] = _dot(l16, l16).astype(BF16)
    for it in range(n_fac):
        for key in units:
            tmat[key] = tmat[key] + _dot(tmat[key].astype(BF16), pw16[key])
        if it + 1 < n_fac:
            for key in units:
                pw16[key] = _dot(pw16[key], pw16[key]).astype(BF16)
    uw = {key: _dot(tmat[key].astype(BF16), per[key]["rhs"]) for key in units}

    st = [state_ref[hd] for hd in range(GDN_HEADS)]
    v_news = {key: [] for key in units}
    o_inter = {key: [] for key in units}
    for s in range(n_sub):
        for c in range(n_chunks):
            lo = c * CHUNK
            for hd in range(GDN_HEADS):
                key = (s, hd)
                u = uw[key][lo:lo + CHUNK, 0:GDN_HEAD_DIM]
                w = uw[key][lo:lo + CHUNK, GDN_HEAD_DIM:2 * GDN_HEAD_DIM].astype(BF16)
                lhs = jnp.concatenate([w, per[key]["qd"][lo:lo + CHUNK]], axis=0)
                res = _dot(lhs, st[hd].astype(BF16))
                v_new = u - res[0:CHUNK]
                o_inter[key].append(res[CHUNK:2 * CHUNK])
                v_news[key].append(v_new.astype(BF16))
                st[hd] = (st[hd] * e_last[s][lo:lo + 1, hd:hd + 1]
                          + _dot_tn(per[key]["kd"][lo:lo + CHUNK], v_news[key][-1]))
    for hd in range(GDN_HEADS):
        state_ref[hd] = st[hd]

    for key in units:
        s, hd = key
        r0 = s * SUB
        c0 = hd * GDN_HEAD_DIM
        o = jnp.concatenate(o_inter[key], axis=0) + _dot(per[key]["attn"], jnp.concatenate(v_news[key], axis=0))
        o = o * lax.rsqrt(jnp.mean(o * o, axis=-1, keepdims=True) + RMS_EPS) * gnw
        zz = z_buf[r0:r0 + SUB, c0:c0 + GDN_HEAD_DIM]
        y_buf[r0:r0 + SUB, c0:c0 + GDN_HEAD_DIM] = (o * _silu(zz)).astype(BF16)


    o_ref[...] = x + gate1 * _dot(y_buf[...], wout_ref[...])


def _ffn_kernel(x_ref, mod_ref, n2_ref, wg_ref, wu_ref, wd_ref, nf_ref, o_ref, a_buf):
    x = x_ref[...]
    shift2 = mod_ref[3:4, :]
    scale2 = mod_ref[4:5, :]
    gate2 = mod_ref[5:6, :]
    ms = jnp.mean(x * x, axis=-1, keepdims=True)
    h = (x * lax.rsqrt(ms + RMS_EPS) * n2_ref[...]) * (1.0 + scale2) + shift2
    hb = h.astype(BF16)
    wcols = D_FF // FF_SPLIT
    for c in range(FF_SPLIT):
        gt = _dot(hb, wg_ref[:, c * wcols:(c + 1) * wcols])
        up = _dot(hb, wu_ref[:, c * wcols:(c + 1) * wcols])
        a_buf[:, c * wcols:(c + 1) * wcols] = (_silu(gt) * up).astype(BF16)
    x2 = x + gate2 * _dot(a_buf[...], wd_ref[...])
    ms2 = jnp.mean(x2 * x2, axis=-1, keepdims=True)
    o_ref[...] = x2 * lax.rsqrt(ms2 + RMS_EPS) * nf_ref[...]


def _const_spec(shape):
    return pl.BlockSpec(shape, lambda *_: (0,) * len(shape))


def _adaln(c, w_ada, b_ada):
    bsz = c.shape[0]
    rows = -(-bsz // SUBLANES) * SUBLANES
    c_pad = jnp.zeros((rows, D_MODEL), F32).at[:bsz].set(c)
    n_out = w_ada.shape[1]
    mod = pl.pallas_call(
        _ada_kernel,
        grid=(n_out // ADA_TN,),
        in_specs=[_const_spec((rows, D_MODEL)),
                  pl.BlockSpec((D_MODEL, ADA_TN), lambda j: (0, j)),
                  pl.BlockSpec((1, ADA_TN), lambda j: (0, j))],
        out_specs=pl.BlockSpec((rows, ADA_TN), lambda j: (0, j)),
        out_shape=jax.ShapeDtypeStruct((rows, n_out), F32),
        compiler_params=pltpu.CompilerParams(dimension_semantics=("arbitrary",), vmem_limit_bytes=VMEM_LIMIT),
        name="adaln_mod",
    )(c_pad, w_ada, b_ada.reshape(1, n_out))
    return mod[:bsz].reshape(bsz, 6, D_MODEL)


def _mixer(x, mod, norm1_w, w_in, conv_qkv_w, a_log, dt_bias, gdn_norm_w, conv_short_w, w_out):
    bsz, seq, _ = x.shape
    gw = GDN_WIDTH
    w16 = w_in.astype(BF16)
    w_qkv = w16[:, 0:3 * gw]
    w_z = w16[:, 3 * gw:4 * gw]
    ab0 = 4 * gw
    w_ab = jnp.zeros((D_MODEL, LANES), BF16).at[:, 0:2 * GDN_HEADS].set(w16[:, ab0:ab0 + 2 * GDN_HEADS])
    sc0 = ab0 + 2 * GDN_HEADS
    w_sc = w16[:, sc0:sc0 + 3 * SC_WIDTH]
    a_log_p = jnp.zeros((1, LANES), F32).at[0, 0:GDN_HEADS].set(a_log)
    dtb_p = jnp.zeros((1, LANES), F32).at[0, 0:GDN_HEADS].set(dt_bias)

    ri = lax.broadcasted_iota(jnp.int32, (SUB, SUB), 0)
    ci = lax.broadcasted_iota(jnp.int32, (SUB, SUB), 1)
    same = (ri // CHUNK) == (ci // CHUNK)
    tril = jnp.where(same & (ri >= ci), 1.0, 0.0).astype(F32)
    strict = jnp.where(same & (ri > ci), 1.0, 0.0).astype(F32)

    tm = TM_MIX
    row_spec = pl.BlockSpec((None, tm, D_MODEL), lambda b, i: (b, i, 0))
    return pl.pallas_call(
        _mixer_kernel,
        grid=(bsz, seq // tm),
        in_specs=[row_spec,
                  pl.BlockSpec((None, 6, D_MODEL), lambda b, i: (b, 0, 0)),
                  _const_spec((1, D_MODEL)),
                  _const_spec((D_MODEL, 3 * gw)),
                  _const_spec((D_MODEL, gw)),
                  _const_spec((D_MODEL, 3 * SC_WIDTH)),
                  _const_spec((D_MODEL, LANES)),
                  _const_spec((GDN_CONV, 3 * gw)),
                  _const_spec((SC_CONV, SC_WIDTH)),
                  _const_spec((1, LANES)),
                  _const_spec((1, LANES)),
                  _const_spec((1, GDN_HEAD_DIM)),
                  _const_spec((D_MODEL, D_MODEL)),
                  _const_spec((SUB, SUB)),
                  _const_spec((SUB, SUB)),
                  _const_spec((SUB, SUB))],
        out_specs=row_spec,
        out_shape=jax.ShapeDtypeStruct(x.shape, F32),
        scratch_shapes=[pltpu.VMEM((tm + SUBLANES, 3 * gw), F32),
                        pltpu.VMEM((tm + SUBLANES, SC_WIDTH), F32),
                        pltpu.VMEM((tm, gw), F32),
                        pltpu.VMEM((tm, SC_WIDTH), F32),
                        pltpu.VMEM((tm, LANES), F32),
                        pltpu.VMEM((tm, D_MODEL), BF16),
                        pltpu.VMEM((GDN_HEADS, GDN_HEAD_DIM, GDN_HEAD_DIM), F32)],
        compiler_params=pltpu.CompilerParams(dimension_semantics=("arbitrary", "arbitrary"),
                                             vmem_limit_bytes=VMEM_LIMIT),
        name="mixer",
    )(x, mod, norm1_w.reshape(1, D_MODEL), w_qkv, w_z, w_sc, w_ab, conv_qkv_w, conv_short_w,
      a_log_p, dtb_p, gdn_norm_w.reshape(1, GDN_HEAD_DIM), w_out.astype(BF16), tril, strict, tril.astype(BF16))


def _ffn(x1, mod, norm2_w, w_gate, w_up, w_down, norm_f_w):
    bsz, seq, _ = x1.shape
    tm = TM_FFN
    row_spec = pl.BlockSpec((None, tm, D_MODEL), lambda b, i: (b, i, 0))
    return pl.pallas_call(
        _ffn_kernel,
        grid=(bsz, seq // tm),
        in_specs=[row_spec,
                  pl.BlockSpec((None, 6, D_MODEL), lambda b, i: (b, 0, 0)),
                  _const_spec((1, D_MODEL)),
                  _const_spec((D_MODEL, D_FF)),
                  _const_spec((D_MODEL, D_FF)),
                  _const_spec((D_FF, D_MODEL)),
                  _const_spec((1, D_MODEL))],
        out_specs=row_spec,
        out_shape=jax.ShapeDtypeStruct(x1.shape, F32),
        scratch_shapes=[pltpu.VMEM((tm, D_FF), BF16)],
        compiler_params=pltpu.CompilerParams(dimension_semantics=("parallel", "parallel"),
                                             vmem_limit_bytes=VMEM_LIMIT),
        name="ffn_final_norm",
    )(x1, mod, norm2_w.reshape(1, D_MODEL), w_gate.astype(BF16), w_up.astype(BF16), w_down.astype(BF16),
      norm_f_w.reshape(1, D_MODEL))


def kernel(x, c, w_ada, b_ada, norm1_w, w_in, conv_qkv_w, A_log, dt_bias, gdn_norm_w, conv_short_w, w_out,
           norm2_w, w_gate, w_up, w_down, norm_f_w):
    depth = w_ada.shape[0]
    for l in range(depth):
        mod = _adaln(c, w_ada[l], b_ada[l])
        x = _mixer(x, mod, norm1_w[l], w_in[l], conv_qkv_w[l], A_log[l], dt_bias[l], gdn_norm_w[l],
                   conv_short_w[l], w_out[l])
        last = l == depth - 1
        assert last, "only the final layer's feed-forward kernel carries the output norm"
        x = _ffn(x, mod, norm2_w[l], w_gate[l], w_up[l], w_down[l], norm_f_w)
    return x
```

```python
import jax
import jax.numpy as jnp
from jax import lax
from jax.experimental import pallas as pl
from jax.experimental.pallas import tpu as pltpu

F32 = jnp.float32
BF16 = jnp.bfloat16

D_MODEL = 1024
GDN_HEADS = 4
GDN_HEAD_DIM = 128
GDN_WIDTH = GDN_HEADS * GDN_HEAD_DIM
SC_WIDTH = 512
D_FF = 2816
CHUNK = 64
GDN_CONV = 4
SC_CONV = 3
RMS_EPS = 1e-6
L2_EPS = 1e-6

LANES = 128
SUBLANES = 8
SUB = 128
TM_FFN = 512
FF_SPLIT = 2
ADA_TN = 1536
VMEM_LIMIT = 56 * 1024 * 1024
NEG_BIG = -1e30


def _dot(a, b):
    return jnp.dot(a, b, preferred_element_type=F32)


def _dot_nt(a, b):
    return lax.dot_general(a, b, (((1,), (1,)), ((), ())), preferred_element_type=F32)


def _dot_tn(a, b):
    return lax.dot_general(a, b, (((0,), (0,)), ((), ())), preferred_element_type=F32)


def _dot_exact_lhs(a_bf16, b_f32):
    b0 = b_f32.astype(BF16)
    r1 = b_f32 - b0.astype(F32)
    b1 = r1.astype(BF16)
    b2 = (r1 - b1.astype(F32)).astype(BF16)
    return _dot(a_bf16, b0) + _dot(a_bf16, b1) + _dot(a_bf16, b2)


def _silu(v):
    return v * jax.nn.sigmoid(v)


def _softplus(v):
    return jnp.maximum(v, 0.0) + jnp.log1p(jnp.exp(-jnp.abs(v)))


def _causal_conv(ext, w_ref, width):
    acc = ext * w_ref[width - 1:width, :]
    for shift in range(1, width):
        acc = acc + pltpu.roll(ext, shift, axis=0) * w_ref[width - 1 - shift:width - shift, :]
    return acc[SUBLANES:, :]


def _ada_kernel(c_ref, w_ref, b_ref, o_ref):
    c = c_ref[...]
    o_ref[...] = _dot(_silu(c).astype(BF16), w_ref[...].astype(BF16)) + b_ref[...]


def _mixer_kernel(x_ref, mod_ref, n1_ref, wqkv_ref, wz_ref, wsc_ref, wab_ref, cq_ref, cs_ref,
                  alog_ref, dtb_ref, gnw_ref, wout_ref, tril_ref, strict_ref, tri16_ref,
                  o_ref, pq_buf, sc_buf, y_buf, state_ref):
    bsz = x_ref.shape[0]
    step = pl.program_id(0)

    @pl.when(step == 0)
    def _():
        pq_buf[:, 0:SUBLANES, :] = jnp.zeros((bsz, SUBLANES, pq_buf.shape[2]), F32)
        sc_buf[:, 0:SUBLANES, :] = jnp.zeros((bsz, SUBLANES, sc_buf.shape[2]), F32)
        state_ref[...] = jnp.zeros(state_ref.shape, F32)

    n1 = n1_ref[...]
    hs = []
    for b in range(bsz):
        xb = x_ref[b]
        ms = jnp.mean(xb * xb, axis=-1, keepdims=True)
        hb = (xb * lax.rsqrt(ms + RMS_EPS) * n1) * (1.0 + mod_ref[b, 1:2, :]) + mod_ref[b, 0:1, :]
        hs.append(hb.astype(BF16))
    h16 = jnp.concatenate(hs, axis=0)

    pq = _dot(h16, wqkv_ref[...])
    zz = _dot(h16, wz_ref[...])
    psc = _dot(h16, wsc_ref[...])
    ab_all = _dot(h16, wab_ref[...])
    for b in range(bsz):
        pq_buf[b, SUBLANES:SUBLANES + SUB, :] = pq[b * SUB:(b + 1) * SUB]
        sc_buf[b, SUBLANES:SUBLANES + SUB, :] = (psc[b * SUB:(b + 1) * SUB, SC_WIDTH:2 * SC_WIDTH]
                                                 * psc[b * SUB:(b + 1) * SUB, 2 * SC_WIDTH:3 * SC_WIDTH])

    a_neg = -jnp.exp(alog_ref[...])
    dtb = dtb_ref[...]
    gnw = gnw_ref[...]
    tril = tril_ref[...]
    strict = strict_ref[...]
    eye = tril - strict
    n_chunks = SUB // CHUNK
    n_fac = CHUNK.bit_length() - 2
    units = [(b, hd) for b in range(bsz) for hd in range(GDN_HEADS)]

    e_last = []
    per = {}
    for b in range(bsz):
        rows = slice(b * SUB, (b + 1) * SUB)
        ext = pq_buf[b]
        qkv = _silu(_causal_conv(ext, cq_ref, GDN_CONV))
        pq_buf[b, 0:SUBLANES, :] = ext[SUB:SUB + SUBLANES]

        ext_sc = sc_buf[b]
        y_sc = psc[rows, 0:SC_WIDTH] * _causal_conv(ext_sc, cs_ref, SC_CONV)
        y_buf[rows, GDN_WIDTH:GDN_WIDTH + SC_WIDTH] = y_sc.astype(BF16)
        sc_buf[b, 0:SUBLANES, :] = ext_sc[SUB:SUB + SUBLANES]

        ab = ab_all[rows]
        g = a_neg * _softplus(ab + dtb)
        beta_all = jax.nn.sigmoid(ab)
        cum = _dot_exact_lhs(tri16_ref[...], g)
        cum_t = cum.T
        ecum = jnp.exp(cum)
        cum_last = jnp.concatenate(
            [jnp.broadcast_to(cum[c * CHUNK + CHUNK - 1:c * CHUNK + CHUNK, :], (CHUNK, LANES))
             for c in range(n_chunks)], axis=0)
        e_to_last = jnp.exp(cum_last - cum)
        e_last.append(jnp.exp(cum_last))

        for hd in range(GDN_HEADS):
            c0 = hd * GDN_HEAD_DIM
            q = qkv[:, c0:c0 + GDN_HEAD_DIM]
            k = qkv[:, GDN_WIDTH + c0:GDN_WIDTH + c0 + GDN_HEAD_DIM]
            v = qkv[:, 2 * GDN_WIDTH + c0:2 * GDN_WIDTH + c0 + GDN_HEAD_DIM]
            qn = q * lax.rsqrt(jnp.sum(q * q, axis=-1, keepdims=True) + L2_EPS) * (GDN_HEAD_DIM ** -0.5)
            kn = k * lax.rsqrt(jnp.sum(k * k, axis=-1, keepdims=True) + L2_EPS)
            beta = beta_all[:, GDN_HEADS + hd:GDN_HEADS + hd + 1]
            eg = ecum[:, hd:hd + 1]
            kb = kn * beta
            diff = cum[:, hd:hd + 1] - cum_t[hd:hd + 1, :]
            dec = jnp.exp(jnp.where(tril > 0.0, diff, NEG_BIG))
            kn16 = kn.astype(BF16)
            lmat = _dot_nt(kb.astype(BF16), kn16) * dec * strict
            per[b, hd] = dict(
                lmat=lmat,
                attn=(_dot_nt(qn.astype(BF16), kn16) * dec).astype(BF16),
                rhs=jnp.concatenate([v * beta, kb * eg], axis=1).astype(BF16),
                qd=(qn * eg).astype(BF16),
                kd=(kn * e_to_last[:, hd:hd + 1]).astype(BF16))

    tmat = {}
    pw16 = {}
    for key in units:
        l16 = per[key]["lmat"].astype(BF16)
        tmat[key] = eye - per[key]["lmat"]
        pw16[key] = _dot(l16, l16).astype(BF16)
    for it in range(n_fac):
        for key in units:
            tmat[key] = tmat[key] + _dot(tmat[key].astype(BF16), pw16[key])
        if it + 1 < n_fac:
            for key in units:
                pw16[key] = _dot(pw16[key], pw16[key]).astype(BF16)
    uw = {key: _dot(tmat[key].astype(BF16), per[key]["rhs"]) for key in units}

    st = {key: state_ref[key[0] * GDN_HEADS + key[1]] for key in units}
    v_news = {key: [] for key in units}
    o_inter = {key: [] for key in units}
    for c in range(n_chunks):
        lo = c * CHUNK
        for key in units:
            b, hd = key
            u = uw[key][lo:lo + CHUNK, 0:GDN_HEAD_DIM]
            w = uw[key][lo:lo + CHUNK, GDN_HEAD_DIM:2 * GDN_HEAD_DIM].astype(BF16)
            lhs = jnp.concatenate([w, per[key]["qd"][lo:lo + CHUNK]], axis=0)
            res = _dot(lhs, st[key].astype(BF16))
            v_new = u - res[0:CHUNK]
            o_inter[key].append(res[CHUNK:2 * CHUNK])
            v_news[key].append(v_new.astype(BF16))
            st[key] = (st[key] * e_last[b][lo:lo + 1, hd:hd + 1]
                       + _dot_tn(per[key]["kd"][lo:lo + CHUNK], v_news[key][-1]))
    for key in units:
        state_ref[key[0] * GDN_HEADS + key[1]] = st[key]

    for key in units:
        b, hd = key
        rows = slice(b * SUB, (b + 1) * SUB)
        c0 = hd * GDN_HEAD_DIM
        o = jnp.concatenate(o_inter[key], axis=0) + _dot(per[key]["attn"], jnp.concatenate(v_news[key], axis=0))
        o = o * lax.rsqrt(jnp.mean(o * o, axis=-1, keepdims=True) + RMS_EPS) * gnw
        y_buf[rows, c0:c0 + GDN_HEAD_DIM] = (o * _silu(zz[rows, c0:c0 + GDN_HEAD_DIM])).astype(BF16)

    mix = _dot(y_buf[...], wout_ref[...])
    for b in range(bsz):
        o_ref[b] = x_ref[b] + mod_ref[b, 2:3, :] * mix[b * SUB:(b + 1) * SUB]


def _ffn_kernel(x_ref, mod_ref, n2_ref, wg_ref, wu_ref, wd_ref, nf_ref, o_ref, a_buf):
    x = x_ref[...]
    shift2 = mod_ref[3:4, :]
    scale2 = mod_ref[4:5, :]
    gate2 = mod_ref[5:6, :]
    ms = jnp.mean(x * x, axis=-1, keepdims=True)
    h = (x * lax.rsqrt(ms + RMS_EPS) * n2_ref[...]) * (1.0 + scale2) + shift2
    hb = h.astype(BF16)
    wcols = D_FF // FF_SPLIT
    for c in range(FF_SPLIT):
        gt = _dot(hb, wg_ref[:, c * wcols:(c + 1) * wcols])
        up = _dot(hb, wu_ref[:, c * wcols:(c + 1) * wcols])
        a_buf[:, c * wcols:(c + 1) * wcols] = (_silu(gt) * up).astype(BF16)
    x2 = x + gate2 * _dot(a_buf[...], wd_ref[...])
    ms2 = jnp.mean(x2 * x2, axis=-1, keepdims=True)
    o_ref[...] = x2 * lax.rsqrt(ms2 + RMS_EPS) * nf_ref[...]


def _const_spec(shape):
    return pl.BlockSpec(shape, lambda *_: (0,) * len(shape))


def _adaln(c, w_ada, b_ada):
    bsz = c.shape[0]
    rows = -(-bsz // SUBLANES) * SUBLANES
    c_pad = jnp.zeros((rows, D_MODEL), F32).at[:bsz].set(c)
    n_out = w_ada.shape[1]
    mod = pl.pallas_call(
        _ada_kernel,
        grid=(n_out // ADA_TN,),
        in_specs=[_const_spec((rows, D_MODEL)),
                  pl.BlockSpec((D_MODEL, ADA_TN), lambda j: (0, j)),
                  pl.BlockSpec((1, ADA_TN), lambda j: (0, j))],
        out_specs=pl.BlockSpec((rows, ADA_TN), lambda j: (0, j)),
        out_shape=jax.ShapeDtypeStruct((rows, n_out), F32),
        compiler_params=pltpu.CompilerParams(dimension_semantics=("arbitrary",), vmem_limit_bytes=VMEM_LIMIT),
        name="adaln_mod",
    )(c_pad, w_ada, b_ada.reshape(1, n_out))
    return mod[:bsz].reshape(bsz, 6, D_MODEL)


def _mixer(x, mod, norm1_w, w_in, conv_qkv_w, a_log, dt_bias, gdn_norm_w, conv_short_w, w_out):
    bsz, seq, _ = x.shape
    gw = GDN_WIDTH
    w16 = w_in.astype(BF16)
    w_qkv = w16[:, 0:3 * gw]
    w_z = w16[:, 3 * gw:4 * gw]
    ab0 = 4 * gw
    w_ab = jnp.zeros((D_MODEL, LANES), BF16).at[:, 0:2 * GDN_HEADS].set(w16[:, ab0:ab0 + 2 * GDN_HEADS])
    sc0 = ab0 + 2 * GDN_HEADS
    w_sc = w16[:, sc0:sc0 + 3 * SC_WIDTH]
    a_log_p = jnp.zeros((1, LANES), F32).at[0, 0:GDN_HEADS].set(a_log)
    dtb_p = jnp.zeros((1, LANES), F32).at[0, 0:GDN_HEADS].set(dt_bias)

    ri = lax.broadcasted_iota(jnp.int32, (SUB, SUB), 0)
    ci = lax.broadcasted_iota(jnp.int32, (SUB, SUB), 1)
    same = (ri // CHUNK) == (ci // CHUNK)
    tril = jnp.where(same & (ri >= ci), 1.0, 0.0).astype(F32)
    strict = jnp.where(same & (ri > ci), 1.0, 0.0).astype(F32)

    row_spec = pl.BlockSpec((bsz, SUB, D_MODEL), lambda i: (0, i, 0))
    return pl.pallas_call(
        _mixer_kernel,
        grid=(seq // SUB,),
        in_specs=[row_spec,
                  _const_spec((bsz, 6, D_MODEL)),
                  _const_spec((1, D_MODEL)),
                  _const_spec((D_MODEL, 3 * gw)),
                  _const_spec((D_MODEL, gw)),
                  _const_spec((D_MODEL, 3 * SC_WIDTH)),
                  _const_spec((D_MODEL, LANES)),
                  _const_spec((GDN_CONV, 3 * gw)),
                  _const_spec((SC_CONV, SC_WIDTH)),
                  _const_spec((1, LANES)),
                  _const_spec((1, LANES)),
                  _const_spec((1, GDN_HEAD_DIM)),
                  _const_spec((D_MODEL, D_MODEL)),
                  _const_spec((SUB, SUB)),
                  _const_spec((SUB, SUB)),
                  _const_spec((SUB, SUB))],
        out_specs=row_spec,
        out_shape=jax.ShapeDtypeStruct(x.shape, F32),
        scratch_shapes=[pltpu.VMEM((bsz, SUB + SUBLANES, 3 * gw), F32),
                        pltpu.VMEM((bsz, SUB + SUBLANES, SC_WIDTH), F32),
                        pltpu.VMEM((bsz * SUB, D_MODEL), BF16),
                        pltpu.VMEM((bsz * GDN_HEADS, GDN_HEAD_DIM, GDN_HEAD_DIM), F32)],
        compiler_params=pltpu.CompilerParams(dimension_semantics=("arbitrary",),
                                             vmem_limit_bytes=VMEM_LIMIT),
        name="mixer",
    )(x, mod, norm1_w.reshape(1, D_MODEL), w_qkv, w_z, w_sc, w_ab, conv_qkv_w, conv_short_w,
      a_log_p, dtb_p, gdn_norm_w.reshape(1, GDN_HEAD_DIM), w_out.astype(BF16), tril, strict, tril.astype(BF16))


def _ffn(x1, mod, norm2_w, w_gate, w_up, w_down, norm_f_w):
    bsz, seq, _ = x1.shape
    tm = TM_FFN
    row_spec = pl.BlockSpec((None, tm, D_MODEL), lambda b, i: (b, i, 0))
    return pl.pallas_call(
        _ffn_kernel,
        grid=(bsz, seq // tm),
        in_specs=[row_spec,
                  pl.BlockSpec((None, 6, D_MODEL), lambda b, i: (b, 0, 0)),
                  _const_spec((1, D_MODEL)),
                  _const_spec((D_MODEL, D_FF)),
                  _const_spec((D_MODEL, D_FF)),
                  _const_spec((D_FF, D_MODEL)),
                  _const_spec((1, D_MODEL))],
        out_specs=row_spec,
        out_shape=jax.ShapeDtypeStruct(x1.shape, F32),
        scratch_shapes=[pltpu.VMEM((tm, D_FF), BF16)],
        compiler_params=pltpu.CompilerParams(dimension_semantics=("parallel", "parallel"),
                                             vmem_limit_bytes=VMEM_LIMIT),
        name="ffn_final_norm",
    )(x1, mod, norm2_w.reshape(1, D_MODEL), w_gate.astype(BF16), w_up.astype(BF16), w_down.astype(BF16),
      norm_f_w.reshape(1, D_MODEL))


def kernel(x, c, w_ada, b_ada, norm1_w, w_in, conv_qkv_w, A_log, dt_bias, gdn_norm_w, conv_short_w, w_out,
           norm2_w, w_gate, w_up, w_down, norm_f_w):
    assert w_ada.shape[0] == 1
    mod = _adaln(c, w_ada[0], b_ada[0])
    x1 = _mixer(x, mod, norm1_w[0], w_in[0], conv_qkv_w[0], A_log[0], dt_bias[0], gdn_norm_w[0],
                conv_short_w[0], w_out[0])
    return _ffn(x1, mod, norm2_w[0], w_gate[0], w_up[0], w_down[0], norm_f_w)
```
